```python
import jax, jax.numpy as jnp
from jax import lax
import numpy as np

D_MODEL = 1024
BATCH = 32
SEQ = 256
DEPTH = 2
DEC_BATCH = 4
DEC_SEQ = 1024
PAST_LEN = 256

GRID_W = 64
N_HEADS = 16
HEAD_DIM = D_MODEL // N_HEADS
WIN_H = 8
WIN_W = 16
Q_BLK_W = 16
K_BLK_W = 32
D_FF = 2816
N_MIXERS = 2
N_SC = (DEPTH + 1) // 2
N_NA = DEPTH // 2
CTX_Q_BLK = 128
EPS = 1e-6
NEG_INF = -1e30

kernel_name = "hybrid_shortconv_natten_diffusion_step"


def rmsnorm(x, g):
    xf = x.astype(jnp.float32)
    y = xf * lax.rsqrt(jnp.mean(xf * xf, axis=-1, keepdims=True) + EPS)
    return y.astype(x.dtype) * g


def adaln_params(cvec, w, b):
    m = jax.nn.silu(cvec) @ w + b
    return jnp.split(m[:, None, :], 6, axis=-1)


def modulate(h, shift, scale):
    return h * (1.0 + scale) + shift


def dwconv3(x, w, b):
    xp = jnp.pad(x, ((0, 0), (1, 1), (0, 0)))
    return xp[:, :-2] * w[0] + xp[:, 1:-1] * w[1] + xp[:, 2:] * w[2] + b


def short_conv_mixer(h, w_in, conv_w, conv_b, w_out):
    bg, cg, xv = jnp.split(h @ w_in, 3, axis=-1)
    return (bg * dwconv3(cg * xv, conv_w, conv_b)) @ w_out


def conv_ffn(h, w_up, conv_w, conv_b, w_down):
    u, g = jnp.split(h @ w_up, 2, axis=-1)
    u = dwconv3(u, conv_w, conv_b)
    return (jax.nn.gelu(u) * g) @ w_down


def split_heads(qkv):
    B, T, _ = qkv.shape
    qkv = qkv.reshape(B, T, 3, N_HEADS, HEAD_DIM).transpose(2, 0, 3, 1, 4)
    return qkv[0], qkv[1], qkv[2]


def merge_heads(o):
    B, H, T, dh = o.shape
    return o.transpose(0, 2, 1, 3).reshape(B, T, H * dh)


def context_attention(q, k, v):
    B, H, L, dh = q.shape
    nb = L // CTX_Q_BLK
    scale = HEAD_DIM ** -0.5
    qb = q.reshape(B, H, nb, CTX_Q_BLK, dh).transpose(2, 0, 1, 3, 4)

    def one_block(qblk):
        s = jnp.einsum('bhqd,bhkd->bhqk', qblk, k).astype(jnp.float32) * scale
        p = jax.nn.softmax(s, axis=-1).astype(v.dtype)
        return jnp.einsum('bhqk,bhkd->bhqd', p, v)

    o = lax.map(one_block, qb)
    return o.transpose(1, 2, 0, 3, 4).reshape(B, H, L, dh)


def neighbourhood_attention(q, k, v, k_ctx, v_ctx, rpb):
    B, H, N, dh = q.shape
    rows = N // GRID_W
    kh = min(WIN_H, rows)
    ncb = GRID_W // Q_BLK_W
    nk = kh * K_BLK_W
    scale = HEAD_DIM ** -0.5
    r = jnp.arange(rows)
    row_idx = jnp.clip(r - kh // 2, 0, rows - kh)[:, None] + jnp.arange(kh)
    j = jnp.arange(ncb)
    kb_start = jnp.clip(j * Q_BLK_W - WIN_W // 2, 0, GRID_W - K_BLK_W)
    col_idx = kb_start[:, None] + jnp.arange(K_BLK_W)
    qcol = j[:, None] * Q_BLK_W + jnp.arange(Q_BLK_W)
    col_start = jnp.clip(qcol - WIN_W // 2, 0, GRID_W - WIN_W)
    kc = col_idx[:, None, :]
    col_mask = (kc >= col_start[..., None]) & (kc < col_start[..., None] + WIN_W)
    dr = row_idx - r[:, None]
    dc = jnp.clip(kc - qcol[..., None], -(WIN_W - 1), WIN_W - 1)
    bias = rpb.astype(jnp.float32)[:, dr[:, None, None, :, None] + (WIN_H - 1),
                                   dc[None, :, :, None, :] + (WIN_W - 1)]
    bias = jnp.where(col_mask[None, None, :, :, None, :], bias, NEG_INF)
    bias = jnp.moveaxis(bias.reshape(H, rows, ncb, Q_BLK_W, nk), 1, 0)
    kg = k.reshape(B, H, rows, GRID_W, dh)
    vg = v.reshape(B, H, rows, GRID_W, dh)
    ri = row_idx[:, None, :, None]
    ci = col_idx[None, :, None, :]
    k_blk = jnp.moveaxis(kg[:, :, ri, ci].reshape(B, H, rows, ncb, nk, dh), 2, 0)
    v_blk = jnp.moveaxis(vg[:, :, ri, ci].reshape(B, H, rows, ncb, nk, dh), 2, 0)
    q_blk = jnp.moveaxis(q.reshape(B, H, rows, ncb, Q_BLK_W, dh), 2, 0)

    def one_row(args):
        q_r, k_r, v_r, b_r = args
        s_loc = jnp.einsum('bhjqd,bhjnd->bhjqn', q_r, k_r).astype(jnp.float32) * scale + b_r[None]
        s_ctx = jnp.einsum('bhjqd,bhld->bhjql', q_r, k_ctx).astype(jnp.float32) * scale
        p = jax.nn.softmax(jnp.concatenate([s_loc, s_ctx], axis=-1), axis=-1).astype(v_r.dtype)
        return (jnp.einsum('bhjqn,bhjnd->bhjqd', p[..., :nk], v_r)
                + jnp.einsum('bhjql,bhld->bhjqd', p[..., nk:], v_ctx))

    o = lax.map(one_row, (q_blk, k_blk, v_blk, bias))
    return jnp.moveaxis(o, 0, 2).reshape(B, H, N, dh)


def setup_inputs(seed: int = 0) -> dict:
    key = jax.random.key(seed)
    ks = jax.random.split(key, 24)
    D = D_MODEL
    nrm = jax.random.normal
    f32 = jnp.float32
    return {
        "x_prompt": nrm(ks[0], (BATCH, SEQ, D), f32),
        "x_sample": nrm(ks[1], (DEC_BATCH, DEC_SEQ, D), f32),
        "cache_k_ctx": nrm(ks[2], (DEC_BATCH, N_NA, N_HEADS, PAST_LEN, HEAD_DIM), f32),
        "cache_v_ctx": nrm(ks[3], (DEC_BATCH, N_NA, N_HEADS, PAST_LEN, HEAD_DIM), f32),
        "c": nrm(ks[4], (DEC_BATCH, D), f32),
        "c_ctx": nrm(ks[5], (D,), f32),
        "ada_w": nrm(ks[6], (DEPTH, D, 6 * D), f32) * (0.5 * D ** -0.5),
        "ada_b": nrm(ks[7], (DEPTH, 6 * D), f32) * 0.02,
        "norm_mix_g": 1.0 + 0.05 * nrm(ks[8], (DEPTH, D), f32),
        "norm_ffn_g": 1.0 + 0.05 * nrm(ks[9], (DEPTH, D), f32),
        "sc_w_in": nrm(ks[10], (N_SC, D, 3 * D), f32) * D ** -0.5,
        "sc_conv_w": nrm(ks[11], (N_SC, 3, D), f32) * 0.5,
        "sc_conv_b": nrm(ks[12], (N_SC, D), f32) * 0.02,
        "sc_w_out": nrm(ks[13], (N_SC, D, D), f32) * D ** -0.5,
        "na_w_qkv": nrm(ks[14], (N_NA, D, 3 * D), f32) * D ** -0.5,
        "na_rpb": nrm(ks[15], (N_NA, N_HEADS, 2 * WIN_H - 1, 2 * WIN_W - 1), f32) * 0.5,
        "na_w_o": nrm(ks[16], (N_NA, D, D), f32) * D ** -0.5,
        "ffn_w_up": nrm(ks[17], (DEPTH, D, 2 * D_FF), f32) * D ** -0.5,
        "ffn_conv_w": nrm(ks[18], (DEPTH, 3, D_FF), f32) * 0.5,
        "ffn_conv_b": nrm(ks[19], (DEPTH, D_FF), f32) * 0.02,
        "ffn_w_down": nrm(ks[20], (DEPTH, D_FF, D), f32) * D_FF ** -0.5,
        "final_g": 1.0 + 0.05 * nrm(ks[21], (D,), f32),
    }


def reference(x_prompt, x_sample, cache_k_ctx, cache_v_ctx, c, c_ctx, ada_w, ada_b,
              norm_mix_g, norm_ffn_g, sc_w_in, sc_conv_w, sc_conv_b, sc_w_out,
              na_w_qkv, na_rpb, na_w_o, ffn_w_up, ffn_conv_w, ffn_conv_b, ffn_w_down, final_g):
    xp = x_prompt
    xs = x_sample
    new_k, new_v = [], []
    for i in range(DEPTH):
        sh1p, sc1p, g1p, sh2p, sc2p, g2p = adaln_params(c_ctx[None, :], ada_w[i], ada_b[i])
        sh1s, sc1s, g1s, sh2s, sc2s, g2s = adaln_params(c, ada_w[i], ada_b[i])
        hp = modulate(rmsnorm(xp, norm_mix_g[i]), sh1p, sc1p)
        hs = modulate(rmsnorm(xs, norm_mix_g[i]), sh1s, sc1s)
        if i % N_MIXERS == 0:
            a = i // N_MIXERS
            yp = short_conv_mixer(hp, sc_w_in[a], sc_conv_w[a], sc_conv_b[a], sc_w_out[a])
            ys = short_conv_mixer(hs, sc_w_in[a], sc_conv_w[a], sc_conv_b[a], sc_w_out[a])
        else:
            b = i // N_MIXERS
            qp, kp, vp = split_heads(hp @ na_w_qkv[b])
            new_k.append(kp)
            new_v.append(vp)
            yp = merge_heads(context_attention(qp, kp, vp)) @ na_w_o[b]
            qs, ksl, vsl = split_heads(hs @ na_w_qkv[b])
            os_ = neighbourhood_attention(qs, ksl, vsl, cache_k_ctx[:, b], cache_v_ctx[:, b], na_rpb[b])
            ys = merge_heads(os_) @ na_w_o[b]
        xp = xp + g1p * yp
        xs = xs + g1s * ys
        hp = modulate(rmsnorm(xp, norm_ffn_g[i]), sh2p, sc2p)
        hs = modulate(rmsnorm(xs, norm_ffn_g[i]), sh2s, sc2s)
        xp = xp + g2p * conv_ffn(hp, ffn_w_up[i], ffn_conv_w[i], ffn_conv_b[i], ffn_w_down[i])
        xs = xs + g2s * conv_ffn(hs, ffn_w_up[i], ffn_conv_w[i], ffn_conv_b[i], ffn_w_down[i])
    y_prompt = rmsnorm(xp, final_g)
    y_sample = rmsnorm(xs, final_g)
    state_k_ctx = jnp.stack(new_k, axis=1)
    state_v_ctx = jnp.stack(new_v, axis=1)
    return (y_prompt, y_sample, state_k_ctx, state_v_ctx)
```

```python
import functools

import jax
import jax.numpy as jnp
from jax import lax
from jax.experimental import pallas as pl
from jax.experimental.pallas import tpu as pltpu

F32 = jnp.float32
BF16 = jnp.bfloat16

D_MODEL = 1024
N_HEADS = 16
HEAD_DIM = D_MODEL // N_HEADS
GRID_W = 64
WIN_H = 8
WIN_W = 16
D_FF = 2816
EPS = 1e-6
NEG_INF = -1e30
N_MOD = 6
N_PAIRS = N_HEADS // 2
PAIR_W = 2 * HEAD_DIM
Q_SCALE = HEAD_DIM ** -0.5

VMEM_LIMIT_BYTES = 56 * 1024 * 1024
SUBLANES = 8

ADALN_TN = 1536
ROWS_PROMPT_TILE = 512
FFN_CHUNKS = ((0, 512), (512, 512), (1024, 512), (1536, 512), (2048, 512), (2560, 256))
SC_CHUNK = 512
NA_KEY_ROWS = WIN_H
NA_CASES = 8


def _dot(a, b):
    return jnp.dot(a, b, preferred_element_type=F32)


def _dot_nt(a, b):
    return lax.dot_general(a, b, (((1,), (1,)), ((), ())), preferred_element_type=F32)


def _rmsnorm(x, g):
    ms = jnp.mean(x * x, axis=-1, keepdims=True)
    return (x * lax.rsqrt(ms + EPS)) * g


def _norm_mod(x, g, shift, scale):
    return _rmsnorm(x, g) * (1.0 + scale) + shift


def _seq_edges(rows, seq_len):
    pos = lax.broadcasted_iota(jnp.int32, (rows, 1), 0) & (seq_len - 1)
    return pos == 0, pos == seq_len - 1


def _dwconv3(z, w, b, is_first, is_last):
    rows = z.shape[0]
    z_prev = jnp.where(is_first, 0.0, pltpu.roll(z, 1, 0))
    z_next = jnp.where(is_last, 0.0, pltpu.roll(z, rows - 1, 0))
    return z_prev * w[0:1, :] + z * w[1:2, :] + z_next * w[2:3, :] + b


def _const_spec(shape):
    nd = len(shape)
    return pl.BlockSpec(shape, lambda *_: (0,) * nd, pipeline_mode=pl.Buffered(1))


def _params(n_axes):
    return pltpu.CompilerParams(
        dimension_semantics=("arbitrary",) * n_axes,
        vmem_limit_bytes=VMEM_LIMIT_BYTES,
    )


def _adaln_kernel(c_ref, w_ref, b_ref, o_ref):
    s = jax.nn.silu(c_ref[...]).astype(BF16)
    o_ref[...] = _dot(s, w_ref[...].astype(BF16)) + b_ref[...]


def _adaln(cvec, ada_w, ada_b):
    depth = ada_w.shape[0]
    n_out = N_MOD * D_MODEL
    out = pl.pallas_call(
        _adaln_kernel,
        grid=(depth, n_out // ADALN_TN),
        in_specs=[
            pl.BlockSpec((SUBLANES, D_MODEL), lambda l, j: (0, 0)),
            pl.BlockSpec((None, D_MODEL, ADALN_TN), lambda l, j: (l, 0, j)),
            pl.BlockSpec((None, 1, ADALN_TN), lambda l, j: (l, 0, j)),
        ],
        out_specs=pl.BlockSpec((None, SUBLANES, ADALN_TN), lambda l, j: (l, 0, j)),
        out_shape=jax.ShapeDtypeStruct((depth, SUBLANES, n_out), F32),
        compiler_params=_params(2),
        name="adaln",
    )(cvec, ada_w, ada_b.reshape(depth, 1, n_out))
    return out.reshape(depth, SUBLANES, N_MOD, D_MODEL)


def _sc_mixer_kernel(x_ref, mod_ref, g_ref, w_in_ref, cw_ref, cb_ref, w_out_ref, o_ref, y_scr, *, seq_len):
    x = x_ref[...]
    rows = x.shape[0]
    h = _norm_mod(x, g_ref[...], mod_ref[0:1, :], mod_ref[1:2, :]).astype(BF16)
    is_first, is_last = _seq_edges(rows, seq_len)
    for off in range(0, D_MODEL, SC_CHUNK):
        sl = slice(off, off + SC_CHUNK)
        bg = _dot(h, w_in_ref[:, off:off + SC_CHUNK])
        cg = _dot(h, w_in_ref[:, D_MODEL + off:D_MODEL + off + SC_CHUNK])
        xv = _dot(h, w_in_ref[:, 2 * D_MODEL + off:2 * D_MODEL + off + SC_CHUNK])
        conv = _dwconv3(cg * xv, cw_ref[:, sl], cb_ref[:, sl], is_first, is_last)
        y_scr[:, sl] = (bg * conv).astype(BF16)
    y = _dot(y_scr[...], w_out_ref[...])
    o_ref[...] = x + mod_ref[2:3, :] * y


def _sc_mixer(x2d, mods, layer, mod_row, g, w_in, cw, cb, w_out, *, tile, seq_len):
    rows = x2d.shape[0]
    kern = functools.partial(_sc_mixer_kernel, seq_len=seq_len)
    return pl.pallas_call(
        kern,
        grid=(rows // tile,),
        in_specs=[
            pl.BlockSpec((tile, D_MODEL), lambda i: (i, 0)),
            pl.BlockSpec((None, None, N_MOD, D_MODEL), lambda i: (layer, mod_row(i), 0, 0)),
            _const_spec((1, D_MODEL)),
            _const_spec((D_MODEL, 3 * D_MODEL)),
            _const_spec((3, D_MODEL)),
            _const_spec((1, D_MODEL)),
            _const_spec((D_MODEL, D_MODEL)),
        ],
        out_specs=pl.BlockSpec((tile, D_MODEL), lambda i: (i, 0)),
        out_shape=jax.ShapeDtypeStruct((rows, D_MODEL), F32),
        scratch_shapes=[pltpu.VMEM((tile, D_MODEL), BF16)],
        compiler_params=_params(1),
        name="sc_mixer",
    )(x2d, mods, g, w_in, cw, cb, w_out)


def _ffn_kernel(x_ref, mod_ref, g_ref, w_up_ref, cw_ref, cb_ref, w_dn_ref, *rest, seq_len, final):
    if final:
        fg_ref, o_ref, acc_scr = rest
    else:
        o_ref, acc_scr = rest
    x = x_ref[...]
    rows = x.shape[0]
    h = _norm_mod(x, g_ref[...], mod_ref[3:4, :], mod_ref[4:5, :]).astype(BF16)
    is_first, is_last = _seq_edges(rows, seq_len)
    for idx, (off, width) in enumerate(FFN_CHUNKS):
        u = _dot(h, w_up_ref[:, off:off + width])
        gate = _dot(h, w_up_ref[:, D_FF + off:D_FF + off + width])
        u = _dwconv3(u, cw_ref[:, off:off + width], cb_ref[:, off:off + width], is_first, is_last)
        act = (jax.nn.gelu(u) * gate).astype(BF16)
        part = _dot(act, w_dn_ref[off:off + width, :])
        if idx == 0:
            acc_scr[...] = part
        else:
            acc_scr[...] += part
    xn = x + mod_ref[5:6, :] * acc_scr[...]
    o_ref[...] = _rmsnorm(xn, fg_ref[...]) if final else xn


def _ffn(x2d, mods, layer, mod_row, g, w_up, cw, cb, w_dn, final_g, *, tile, seq_len):
    rows = x2d.shape[0]
    final = final_g is not None
    kern = functools.partial(_ffn_kernel, seq_len=seq_len, final=final)
    in_specs = [
        pl.BlockSpec((tile, D_MODEL), lambda i: (i, 0)),
        pl.BlockSpec((None, None, N_MOD, D_MODEL), lambda i: (layer, mod_row(i), 0, 0)),
        _const_spec((1, D_MODEL)),
        _const_spec((D_MODEL, 2 * D_FF)),
        _const_spec((3, D_FF)),
        _const_spec((1, D_FF)),
        _const_spec((D_FF, D_MODEL)),
    ]
    args = [x2d, mods, g, w_up, cw, cb, w_dn]
    if final:
        in_specs.append(_const_spec((1, D_MODEL)))
        args.append(final_g)
    return pl.pallas_call(
        kern,
        grid=(rows // tile,),
        in_specs=in_specs,
        out_specs=pl.BlockSpec((tile, D_MODEL), lambda i: (i, 0)),
        out_shape=jax.ShapeDtypeStruct((rows, D_MODEL), F32),
        scratch_shapes=[pltpu.VMEM((tile, D_MODEL), F32)],
        compiler_params=_params(1),
        name="conv_ffn",
    )(*args)


def _lane_is_first_head():
    return lax.broadcasted_iota(jnp.int32, (1, PAIR_W), 1) < HEAD_DIM


def _softmax_pv(scores, values):
    m = functools.reduce(jnp.maximum, [jnp.max(s, axis=-1, keepdims=True) for s in scores])
    es = [jnp.exp(s - m) for s in scores]
    denom = functools.reduce(jnp.add, [jnp.sum(e, axis=-1, keepdims=True) for e in es])
    out = functools.reduce(jnp.add, [_dot(e.astype(BF16), v) for e, v in zip(es, values)])
    return out * (1.0 / denom)


def _ctx_mixer_kernel(x_ref, mod_ref, g_ref, w_qkv_ref, w_o_ref, o_ref, k_ref, v_ref, o_scr):
    x = x_ref[...]
    h = _norm_mod(x, g_ref[...], mod_ref[0:1, :], mod_ref[1:2, :]).astype(BF16)
    q = _dot(h, w_qkv_ref[:, 0:D_MODEL]) * Q_SCALE
    k = _dot(h, w_qkv_ref[:, D_MODEL:2 * D_MODEL])
    v = _dot(h, w_qkv_ref[:, 2 * D_MODEL:3 * D_MODEL])
    for hd in range(N_HEADS):
        k_ref[hd] = k[:, hd * HEAD_DIM:(hd + 1) * HEAD_DIM]
        v_ref[hd] = v[:, hd * HEAD_DIM:(hd + 1) * HEAD_DIM]
    first = _lane_is_first_head()
    for p in range(N_PAIRS):
        sl = slice(p * PAIR_W, (p + 1) * PAIR_W)
        q2 = q[:, sl]
        k2 = k[:, sl].astype(BF16)
        v2 = v[:, sl].astype(BF16)
        qa = jnp.where(first, q2, 0.0).astype(BF16)
        qb = jnp.where(first, 0.0, q2).astype(BF16)
        oa = _softmax_pv([_dot_nt(qa, k2)], [v2])
        ob = _softmax_pv([_dot_nt(qb, k2)], [v2])
        o_scr[:, sl] = jnp.where(first, oa, ob).astype(BF16)
    y = _dot(o_scr[...], w_o_ref[...])
    o_ref[...] = x + mod_ref[2:3, :] * y


def _ctx_mixer(x, mods, layer, g, w_qkv, w_o):
    batch, seq, _ = x.shape
    kv_shape = jax.ShapeDtypeStruct((batch, 1, N_HEADS, seq, HEAD_DIM), F32)
    kv_spec = pl.BlockSpec((None, None, N_HEADS, seq, HEAD_DIM), lambda b: (b, 0, 0, 0, 0))
    return pl.pallas_call(
        _ctx_mixer_kernel,
        grid=(batch,),
        in_specs=[
            pl.BlockSpec((None, seq, D_MODEL), lambda b: (b, 0, 0)),
            pl.BlockSpec((None, None, N_MOD, D_MODEL), lambda b: (layer, 0, 0, 0)),
            _const_spec((1, D_MODEL)),
            _const_spec((D_MODEL, 3 * D_MODEL)),
            _const_spec((D_MODEL, D_MODEL)),
        ],
        out_specs=[pl.BlockSpec((None, seq, D_MODEL), lambda b: (b, 0, 0)), kv_spec, kv_spec],
        out_shape=[jax.ShapeDtypeStruct(x.shape, F32), kv_shape, kv_shape],
        scratch_shapes=[pltpu.VMEM((seq, D_MODEL), BF16)],
        compiler_params=_params(1),
        name="ctx_mixer",
    )(x, mods, g, w_qkv, w_o)


def _na_bias_table(rpb):
    rows = 16
    qcol = jnp.arange(GRID_W)[:, None]
    kcol = jnp.arange(GRID_W)[None, :]
    col_start = jnp.clip(qcol - WIN_W // 2, 0, GRID_W - WIN_W)
    in_window = (kcol >= col_start) & (kcol < col_start + WIN_W)
    dc = jnp.clip(kcol - qcol, -(WIN_W - 1), WIN_W - 1) + (WIN_W - 1)
    case_row = jnp.array([0, 1, 2, 3, 4, rows - 3, rows - 2, rows - 1])
    row0 = jnp.clip(case_row - WIN_H // 2, 0, rows - WIN_H)
    dr = row0[:, None] + jnp.arange(NA_KEY_ROWS)[None, :] - case_row[:, None] + (WIN_H - 1)
    table = rpb.astype(F32)[:, dr[:, None, :, None], dc[None, :, None, :]]
    table = jnp.where(in_window[None, None, :, None, :], table, NEG_INF)
    return table.reshape(rpb.shape[0], NA_CASES, GRID_W, NA_KEY_ROWS * GRID_W)


def _na_mixer_kernel(x_ref, mod_ref, g_ref, w_qkv_ref, w_o_ref, kc_ref, vc_ref, bias_ref, o_ref,
                     q_scr, k_scr, v_scr, o_scr):
    p = pl.program_id(1)
    n_rows = x_ref.shape[0] // GRID_W

    @pl.when(p == 0)
    def _project():
        h = _norm_mod(x_ref[...], g_ref[...], mod_ref[0:1, :], mod_ref[1:2, :]).astype(BF16)
        for dst, base, scale in ((q_scr, 0, Q_SCALE), (k_scr, D_MODEL, None), (v_scr, 2 * D_MODEL, None)):
            t = _dot(h, w_qkv_ref[:, base:base + D_MODEL])
            if scale is not None:
                t = t * scale
            for j in range(N_PAIRS):
                dst[j] = t[:, j * PAIR_W:(j + 1) * PAIR_W].astype(BF16)

    first = _lane_is_first_head()
    kc2 = kc_ref[...]
    vc2 = vc_ref[...]
    span = NA_KEY_ROWS * GRID_W

    def row_body(r, carry):
        row0 = jnp.clip(r - WIN_H // 2, 0, n_rows - WIN_H)
        case = jnp.where(r < WIN_H // 2, r, jnp.where(r > n_rows - WIN_H // 2, r - (n_rows - NA_CASES), WIN_H // 2))
        q_rows = pl.ds(pl.multiple_of(r * GRID_W, GRID_W), GRID_W)
        k_rows = pl.ds(pl.multiple_of(row0 * GRID_W, GRID_W), span)
        q2 = q_scr[p, q_rows, :]
        k2 = k_scr[p, k_rows, :]
        v2 = v_scr[p, k_rows, :]
        outs = []
        for half, qm in enumerate((jnp.where(first, q2, jnp.zeros_like(q2)), jnp.where(first, jnp.zeros_like(q2), q2))):
            s_loc = _dot_nt(qm, k2) + bias_ref[half, case]
            s_ctx = _dot_nt(qm, kc2)
            outs.append(_softmax_pv([s_loc, s_ctx], [v2, vc2]))
        o_scr[p, q_rows, :] = jnp.where(first, outs[0], outs[1]).astype(BF16)
        return carry

    lax.fori_loop(0, n_rows, row_body, 0)

    @pl.when(p == N_PAIRS - 1)
    def _output():
        o_all = jnp.concatenate([o_scr[j] for j in range(N_PAIRS)], axis=1)
        y = _dot(o_all, w_o_ref[...])
        o_ref[...] = x_ref[...] + mod_ref[2:3, :] * y


def _na_mixer(x, mods, layer, g, w_qkv, w_o, kc, vc, bias):
    batch, seq, _ = x.shape
    pair_scr = pltpu.VMEM((N_PAIRS, seq, PAIR_W), BF16)
    return pl.pallas_call(
        _na_mixer_kernel,
        grid=(batch, N_PAIRS),
        in_specs=[
            pl.BlockSpec((None, seq, D_MODEL), lambda b, p: (b, 0, 0)),
            pl.BlockSpec((None, None, N_MOD, D_MODEL), lambda b, p: (layer, 1 + b, 0, 0)),
            _const_spec((1, D_MODEL)),
            _const_spec((D_MODEL, 3 * D_MODEL)),
            _const_spec((D_MODEL, D_MODEL)),
            pl.BlockSpec((None, None) + kc.shape[2:], lambda b, p: (b, p, 0, 0)),
            pl.BlockSpec((None, None) + vc.shape[2:], lambda b, p: (b, p, 0, 0)),
            pl.BlockSpec((2,) + bias.shape[1:], lambda b, p: (p, 0, 0, 0)),
        ],
        out_specs=pl.BlockSpec((None, seq, D_MODEL), lambda b, p: (b, 0, 0)),
        out_shape=jax.ShapeDtypeStruct(x.shape, F32),
        scratch_shapes=[pair_scr, pair_scr, pair_scr, pair_scr],
        compiler_params=_params(2),
        name="na_mixer",
    )(x, mods, g, w_qkv, w_o, kc, vc, bias)


def _pair_layout(cache):
    b, h, l, dh = cache.shape
    return cache.reshape(b, h // 2, 2, l, dh).transpose(0, 1, 3, 2, 4).reshape(b, h // 2, l, 2 * dh).astype(BF16)


def kernel(x_prompt, x_sample, cache_k_ctx, cache_v_ctx, c, c_ctx, ada_w, ada_b, norm_mix_g, norm_ffn_g,
           sc_w_in, sc_conv_w, sc_conv_b, sc_w_out, na_w_qkv, na_rpb, na_w_o, ffn_w_up, ffn_conv_w,
           ffn_conv_b, ffn_w_down, final_g):
    batch, seq, _ = x_prompt.shape
    dec_batch, dec_seq, _ = x_sample.shape
    assert ROWS_PROMPT_TILE % seq == 0 and dec_seq % GRID_W == 0 and dec_seq // GRID_W == 16

    cvec = jnp.concatenate([c_ctx[None, :], c, jnp.zeros((SUBLANES - 1 - dec_batch, D_MODEL), F32)], axis=0)
    mods = _adaln(cvec, ada_w, ada_b)

    prompt_row = lambda i: 0
    sample_row = lambda i: 1 + i
    xp = x_prompt.reshape(batch * seq, D_MODEL)
    xs = x_sample.reshape(dec_batch * dec_seq, D_MODEL)

    def sc_layer(xp, xs, layer, a):
        args = (norm_mix_g[layer][None, :], sc_w_in[a].astype(BF16), sc_conv_w[a], sc_conv_b[a][None, :],
                sc_w_out[a].astype(BF16))
        return (_sc_mixer(xp, mods, layer, prompt_row, *args, tile=ROWS_PROMPT_TILE, seq_len=seq),
                _sc_mixer(xs, mods, layer, sample_row, *args, tile=dec_seq, seq_len=dec_seq))

    def ffn_layer(xp, xs, layer, fg):
        args = (norm_ffn_g[layer][None, :], ffn_w_up[layer].astype(BF16), ffn_conv_w[layer],
                ffn_conv_b[layer][None, :], ffn_w_down[layer].astype(BF16))
        return (_ffn(xp, mods, layer, prompt_row, *args, fg, tile=ROWS_PROMPT_TILE, seq_len=seq),
                _ffn(xs, mods, layer, sample_row, *args, fg, tile=dec_seq, seq_len=dec_seq))

    depth = ada_w.shape[0]
    new_k, new_v = [], []
    for layer in range(depth):
        if layer % 2 == 0:
            xp, xs = sc_layer(xp, xs, layer, layer // 2)
        else:
            b = layer // 2
            g = norm_mix_g[layer][None, :]
            w_qkv = na_w_qkv[b].astype(BF16)
            w_o = na_w_o[b].astype(BF16)
            xp3, k_new, v_new = _ctx_mixer(xp.reshape(batch, seq, D_MODEL), mods, layer, g, w_qkv, w_o)
            new_k.append(k_new)
            new_v.append(v_new)
            xs3 = _na_mixer(xs.reshape(dec_batch, dec_seq, D_MODEL), mods, layer, g, w_qkv, w_o,
                            _pair_layout(cache_k_ctx[:, b]), _pair_layout(cache_v_ctx[:, b]),
                            _na_bias_table(na_rpb[b]))
            xp = xp3.reshape(batch * seq, D_MODEL)
            xs = xs3.reshape(dec_batch * dec_seq, D_MODEL)
        fg = final_g[None, :] if layer == depth - 1 else None
        xp, xs = ffn_layer(xp, xs, layer, fg)

    y_prompt = xp.reshape(batch, seq, D_MODEL)
    y_sample = xs.reshape(dec_batch, dec_seq, D_MODEL)
    return (y_prompt, y_sample, jnp.concatenate(new_k, axis=1), jnp.concatenate(new_v, axis=1))
```

```python
import functools

import jax
import jax.numpy as jnp
from jax import lax
from jax.experimental import pallas as pl
from jax.experimental.pallas import tpu as pltpu

F32 = jnp.float32
BF16 = jnp.bfloat16

D_MODEL = 1024
N_HEADS = 16
HEAD_DIM = D_MODEL // N_HEADS
GRID_W = 64
WIN_H = 8
WIN_W = 16
D_FF = 2816
EPS = 1e-6
NEG_INF = -1e30
N_MOD = 6
N_PAIRS = N_HEADS // 2
PAIR_W = 2 * HEAD_DIM
Q_SCALE = HEAD_DIM ** -0.5

VMEM_LIMIT_BYTES = 56 * 1024 * 1024
SUBLANES = 8

ADALN_TN = 1536
ROWS_PROMPT_TILE = 512
FFN_CHUNKS = ((0, 512), (512, 512), (1024, 512), (1536, 512), (2048, 512), (2560, 256))
SC_CHUNK = 512
NA_KEY_ROWS = WIN_H


def _dot(a, b):
    return jnp.dot(a, b, preferred_element_type=F32)


def _dot_nt(a, b):
    return lax.dot_general(a, b, (((1,), (1,)), ((), ())), preferred_element_type=F32)


def _rmsnorm(x, g):
    ms = jnp.mean(x * x, axis=-1, keepdims=True)
    return (x * lax.rsqrt(ms + EPS)) * g


def _norm_mod(x, g, shift, scale):
    return _rmsnorm(x, g) * (1.0 + scale) + shift


def _seq_edges(rows, seq_len):
    pos = lax.broadcasted_iota(jnp.int32, (rows, 1), 0) & (seq_len - 1)
    return pos == 0, pos == seq_len - 1


def _dwconv3(z, w, b, is_first, is_last):
    rows = z.shape[0]
    z_prev = jnp.where(is_first, 0.0, pltpu.roll(z, 1, 0))
    z_next = jnp.where(is_last, 0.0, pltpu.roll(z, rows - 1, 0))
    return z_prev * w[0:1, :] + z * w[1:2, :] + z_next * w[2:3, :] + b


def _const_spec(shape):
    nd = len(shape)
    return pl.BlockSpec(shape, lambda *_: (0,) * nd, pipeline_mode=pl.Buffered(1))


def _params(n_axes):
    return pltpu.CompilerParams(
        dimension_semantics=("arbitrary",) * n_axes,
        vmem_limit_bytes=VMEM_LIMIT_BYTES,
    )


def _adaln_kernel(c_ref, w_ref, b_ref, o_ref):
    s = jax.nn.silu(c_ref[...]).astype(BF16)
    o_ref[...] = _dot(s, w_ref[...].astype(BF16)) + b_ref[...]


def _adaln(cvec, ada_w, ada_b):
    depth = ada_w.shape[0]
    n_out = N_MOD * D_MODEL
    out = pl.pallas_call(
        _adaln_kernel,
        grid=(depth, n_out // ADALN_TN),
        in_specs=[
            pl.BlockSpec((SUBLANES, D_MODEL), lambda l, j: (0, 0)),
            pl.BlockSpec((None, D_MODEL, ADALN_TN), lambda l, j: (l, 0, j)),
            pl.BlockSpec((None, 1, ADALN_TN), lambda l, j: (l, 0, j)),
        ],
        out_specs=pl.BlockSpec((None, SUBLANES, ADALN_TN), lambda l, j: (l, 0, j)),
        out_shape=jax.ShapeDtypeStruct((depth, SUBLANES, n_out), F32),
        compiler_params=_params(2),
        name="adaln",
    )(cvec, ada_w, ada_b.reshape(depth, 1, n_out))
    return out.reshape(depth, SUBLANES, N_MOD, D_MODEL)


def _sc_mixer_kernel(x_ref, mod_ref, g_ref, w_in_ref, cw_ref, cb_ref, w_out_ref, o_ref, y_scr, *, seq_len):
    x = x_ref[...]
    rows = x.shape[0]
    h = _norm_mod(x, g_ref[...], mod_ref[0:1, :], mod_ref[1:2, :]).astype(BF16)
    is_first, is_last = _seq_edges(rows, seq_len)
    for off in range(0, D_MODEL, SC_CHUNK):
        sl = slice(off, off + SC_CHUNK)
        bg = _dot(h, w_in_ref[:, off:off + SC_CHUNK])
        cg = _dot(h, w_in_ref[:, D_MODEL + off:D_MODEL + off + SC_CHUNK])
        xv = _dot(h, w_in_ref[:, 2 * D_MODEL + off:2 * D_MODEL + off + SC_CHUNK])
        conv = _dwconv3(cg * xv, cw_ref[:, sl], cb_ref[:, sl], is_first, is_last)
        y_scr[:, sl] = (bg * conv).astype(BF16)
    y = _dot(y_scr[...], w_out_ref[...])
    o_ref[...] = x + mod_ref[2:3, :] * y


def _sc_mixer(x2d, mods, layer, mod_row, g, w_in, cw, cb, w_out, *, tile, seq_len):
    rows = x2d.shape[0]
    kern = functools.partial(_sc_mixer_kernel, seq_len=seq_len)
    return pl.pallas_call(
        kern,
        grid=(rows // tile,),
        in_specs=[
            pl.BlockSpec((tile, D_MODEL), lambda i: (i, 0)),
            pl.BlockSpec((None, None, N_MOD, D_MODEL), lambda i: (layer, mod_row(i), 0, 0)),
            _const_spec((1, D_MODEL)),
            _const_spec((D_MODEL, 3 * D_MODEL)),
            _const_spec((3, D_MODEL)),
            _const_spec((1, D_MODEL)),
            _const_spec((D_MODEL, D_MODEL)),
        ],
        out_specs=pl.BlockSpec((tile, D_MODEL), lambda i: (i, 0)),
        out_shape=jax.ShapeDtypeStruct((rows, D_MODEL), F32),
        scratch_shapes=[pltpu.VMEM((tile, D_MODEL), BF16)],
        compiler_params=_params(1),
        name="sc_mixer",
    )(x2d, mods, g, w_in, cw, cb, w_out)


def _ffn_kernel(x_ref, mod_ref, g_ref, w_up_ref, cw_ref, cb_ref, w_dn_ref, *rest, seq_len, final):
    if final:
        fg_ref, o_ref, acc_scr = rest
    else:
        o_ref, acc_scr = rest
    x = x_ref[...]
    rows = x.shape[0]
    h = _norm_mod(x, g_ref[...], mod_ref[3:4, :], mod_ref[4:5, :]).astype(BF16)
    is_first, is_last = _seq_edges(rows, seq_len)
    for idx, (off, width) in enumerate(FFN_CHUNKS):
        u = _dot(h, w_up_ref[:, off:off + width])
        gate = _dot(h, w_up_ref[:, D_FF + off:D_FF + off + width])
        u = _dwconv3(u, cw_ref[:, off:off + width], cb_ref[:, off:off + width], is_first, is_last)
        act = (jax.nn.gelu(u) * gate).astype(BF16)
        part = _dot(act, w_dn_ref[off:off + width, :])
        if idx == 0:
            acc_scr[...] = part
        else:
            acc_scr[...] += part
    xn = x + mod_ref[5:6, :] * acc_scr[...]
    o_ref[...] = _rmsnorm(xn, fg_ref[...]) if final else xn


def _ffn(x2d, mods, layer, mod_row, g, w_up, cw, cb, w_dn, final_g, *, tile, seq_len):
    rows = x2d.shape[0]
    final = final_g is not None
    kern = functools.partial(_ffn_kernel, seq_len=seq_len, final=final)
    in_specs = [
        pl.BlockSpec((tile, D_MODEL), lambda i: (i, 0)),
        pl.BlockSpec((None, None, N_MOD, D_MODEL), lambda i: (layer, mod_row(i), 0, 0)),
        _const_spec((1, D_MODEL)),
        _const_spec((D_MODEL, 2 * D_FF)),
        _const_spec((3, D_FF)),
        _const_spec((1, D_FF)),
        _const_spec((D_FF, D_MODEL)),
    ]
    args = [x2d, mods, g, w_up, cw, cb, w_dn]
    if final:
        in_specs.append(_const_spec((1, D_MODEL)))
        args.append(final_g)
    return pl.pallas_call(
        kern,
        grid=(rows // tile,),
        in_specs=in_specs,
        out_specs=pl.BlockSpec((tile, D_MODEL), lambda i: (i, 0)),
        out_shape=jax.ShapeDtypeStruct((rows, D_MODEL), F32),
        scratch_shapes=[pltpu.VMEM((tile, D_MODEL), F32)],
        compiler_params=_params(1),
        name="conv_ffn",
    )(*args)


def _lane_is_first_head():
    return lax.broadcasted_iota(jnp.int32, (1, PAIR_W), 1) < HEAD_DIM


def _softmax_pv(scores, values):
    m = functools.reduce(jnp.maximum, [jnp.max(s, axis=-1, keepdims=True) for s in scores])
    es = [jnp.exp(s - m) for s in scores]
    denom = functools.reduce(jnp.add, [jnp.sum(e, axis=-1, keepdims=True) for e in es])
    out = functools.reduce(jnp.add, [_dot(e.astype(BF16), v) for e, v in zip(es, values)])
    return out * (1.0 / denom)


def _ctx_mixer_kernel(x_ref, mod_ref, g_ref, w_qkv_ref, w_o_ref, o_ref, k_ref, v_ref, o_scr):
    x = x_ref[...]
    h = _norm_mod(x, g_ref[...], mod_ref[0:1, :], mod_ref[1:2, :]).astype(BF16)
    q = _dot(h, w_qkv_ref[:, 0:D_MODEL]) * Q_SCALE
    k = _dot(h, w_qkv_ref[:, D_MODEL:2 * D_MODEL])
    v = _dot(h, w_qkv_ref[:, 2 * D_MODEL:3 * D_MODEL])
    for hd in range(N_HEADS):
        k_ref[hd] = k[:, hd * HEAD_DIM:(hd + 1) * HEAD_DIM]
        v_ref[hd] = v[:, hd * HEAD_DIM:(hd + 1) * HEAD_DIM]
    first = _lane_is_first_head()
    for p in range(N_PAIRS):
        sl = slice(p * PAIR_W, (p + 1) * PAIR_W)
        q2 = q[:, sl]
        k2 = k[:, sl].astype(BF16)
        v2 = v[:, sl].astype(BF16)
        qa = jnp.where(first, q2, 0.0).astype(BF16)
        qb = jnp.where(first, 0.0, q2).astype(BF16)
        oa = _softmax_pv([_dot_nt(qa, k2)], [v2])
        ob = _softmax_pv([_dot_nt(qb, k2)], [v2])
        o_scr[:, sl] = jnp.where(first, oa, ob).astype(BF16)
    y = _dot(o_scr[...], w_o_ref[...])
    o_ref[...] = x + mod_ref[2:3, :] * y


def _ctx_mixer(x, mods, layer, g, w_qkv, w_o):
    batch, seq, _ = x.shape
    kv_shape = jax.ShapeDtypeStruct((batch, 1, N_HEADS, seq, HEAD_DIM), F32)
    kv_spec = pl.BlockSpec((None, None, N_HEADS, seq, HEAD_DIM), lambda b: (b, 0, 0, 0, 0))
    return pl.pallas_call(
        _ctx_mixer_kernel,
        grid=(batch,),
        in_specs=[
            pl.BlockSpec((None, seq, D_MODEL), lambda b: (b, 0, 0)),
            pl.BlockSpec((None, None, N_MOD, D_MODEL), lambda b: (layer, 0, 0, 0)),
            _const_spec((1, D_MODEL)),
            _const_spec((D_MODEL, 3 * D_MODEL)),
            _const_spec((D_MODEL, D_MODEL)),
        ],
        out_specs=[pl.BlockSpec((None, seq, D_MODEL), lambda b: (b, 0, 0)), kv_spec, kv_spec],
        out_shape=[jax.ShapeDtypeStruct(x.shape, F32), kv_shape, kv_shape],
        scratch_shapes=[pltpu.VMEM((seq, D_MODEL), BF16)],
        compiler_params=_params(1),
        name="ctx_mixer",
    )(x, mods, g, w_qkv, w_o)


def _na_bias_table(rpb):
    n_h, n_dr, n_dc = rpb.shape
    width = 2 * GRID_W
    pad_lo = GRID_W - WIN_W
    pad_hi = width - pad_lo - n_dc
    rpb = rpb.astype(F32)
    ext = jnp.concatenate([jnp.broadcast_to(rpb[..., :1], (n_h, n_dr, pad_lo)), rpb,
                           jnp.broadcast_to(rpb[..., -1:], (n_h, n_dr, pad_hi))], axis=-1)
    skew = jnp.broadcast_to(ext[:, :, None, :], (n_h, n_dr, GRID_W, width)).reshape(n_h, n_dr, GRID_W * width)
    skew = skew[..., :GRID_W * (width - 1)].reshape(n_h, n_dr, GRID_W, width - 1)
    toeplitz = skew[..., GRID_W - 1:2 * GRID_W - 1]
    qcol = jnp.arange(GRID_W)[:, None]
    kcol = jnp.arange(GRID_W)[None, :]
    col_start = jnp.clip(qcol - WIN_W // 2, 0, GRID_W - WIN_W)
    in_window = (kcol >= col_start) & (kcol < col_start + WIN_W)
    table = jnp.where(in_window[None, None], toeplitz, NEG_INF)
    return jnp.concatenate([table[:, :-1], table[:, 1:]], axis=-1)


def _na_mixer_kernel(x_ref, mod_ref, g_ref, w_qkv_ref, w_o_ref, kc_ref, vc_ref, bias_ref, o_ref,
                     q_scr, k_scr, v_scr, o_scr):
    p = pl.program_id(1)
    n_rows = x_ref.shape[0] // GRID_W

    @pl.when(p == 0)
    def _project():
        h = _norm_mod(x_ref[...], g_ref[...], mod_ref[0:1, :], mod_ref[1:2, :]).astype(BF16)
        for dst, base, scale in ((q_scr, 0, Q_SCALE), (k_scr, D_MODEL, None), (v_scr, 2 * D_MODEL, None)):
            t = _dot(h, w_qkv_ref[:, base:base + D_MODEL])
            if scale is not None:
                t = t * scale
            for j in range(N_PAIRS):
                dst[j] = t[:, j * PAIR_W:(j + 1) * PAIR_W].astype(BF16)

    first = _lane_is_first_head()
    kc2 = kc_ref[...]
    vc2 = vc_ref[...]
    span = NA_KEY_ROWS * GRID_W

    def row_body(r, carry):
        row0 = jnp.clip(r - WIN_H // 2, 0, n_rows - WIN_H)
        d0 = row0 - r + (WIN_H - 1)
        q_rows = pl.ds(pl.multiple_of(r * GRID_W, GRID_W), GRID_W)
        k_rows = pl.ds(pl.multiple_of(row0 * GRID_W, GRID_W), span)
        q2 = q_scr[p, q_rows, :]
        k2 = k_scr[p, k_rows, :]
        v2 = v_scr[p, k_rows, :]
        outs = []
        for half, qm in enumerate((jnp.where(first, q2, jnp.zeros_like(q2)), jnp.where(first, jnp.zeros_like(q2), q2))):
            bias = jnp.concatenate([bias_ref[half, d0 + 2 * m] for m in range(NA_KEY_ROWS // 2)], axis=1)
            s_loc = _dot_nt(qm, k2) + bias
            s_ctx = _dot_nt(qm, kc2)
            outs.append(_softmax_pv([s_loc, s_ctx], [v2, vc2]))
        o_scr[p, q_rows, :] = jnp.where(first, outs[0], outs[1]).astype(BF16)
        return carry

    lax.fori_loop(0, n_rows, row_body, 0)

    @pl.when(p == N_PAIRS - 1)
    def _output():
        o_all = jnp.concatenate([o_scr[j] for j in range(N_PAIRS)], axis=1)
        y = _dot(o_all, w_o_ref[...])
        o_ref[...] = x_ref[...] + mod_ref[2:3, :] * y


def _na_mixer(x, mods, layer, g, w_qkv, w_o, kc, vc, bias):
    batch, seq, _ = x.shape
    pair_scr = pltpu.VMEM((N_PAIRS, seq, PAIR_W), BF16)
    return pl.pallas_call(
        _na_mixer_kernel,
        grid=(batch, N_PAIRS),
        in_specs=[
            pl.BlockSpec((None, seq, D_MODEL), lambda b, p: (b, 0, 0)),
            pl.BlockSpec((None, None, N_MOD, D_MODEL), lambda b, p: (layer, 1 + b, 0, 0)),
            _const_spec((1, D_MODEL)),
            _const_spec((D_MODEL, 3 * D_MODEL)),
            _const_spec((D_MODEL, D_MODEL)),
            pl.BlockSpec((None, None) + kc.shape[2:], lambda b, p: (b, p, 0, 0)),
            pl.BlockSpec((None, None) + vc.shape[2:], lambda b, p: (b, p, 0, 0)),
            pl.BlockSpec((2,) + bias.shape[1:], lambda b, p: (p, 0, 0, 0)),
        ],
        out_specs=pl.BlockSpec((None, seq, D_MODEL), lambda b, p: (b, 0, 0)),
        out_shape=jax.ShapeDtypeStruct(x.shape, F32),
        scratch_shapes=[pair_scr, pair_scr, pair_scr, pair_scr],
        compiler_params=_params(2),
        name="na_mixer",
    )(x, mods, g, w_qkv, w_o, kc, vc, bias)


def _pair_layout(cache):
    b, h, l, dh = cache.shape
    return cache.reshape(b, h // 2, 2, l, dh).transpose(0, 1, 3, 2, 4).reshape(b, h // 2, l, 2 * dh).astype(BF16)


def kernel(x_prompt, x_sample, cache_k_ctx, cache_v_ctx, c, c_ctx, ada_w, ada_b, norm_mix_g, norm_ffn_g,
           sc_w_in, sc_conv_w, sc_conv_b, sc_w_out, na_w_qkv, na_rpb, na_w_o, ffn_w_up, ffn_conv_w,
           ffn_conv_b, ffn_w_down, final_g):
    batch, seq, _ = x_prompt.shape
    dec_batch, dec_seq, _ = x_sample.shape
    assert ROWS_PROMPT_TILE % seq == 0 and dec_seq % GRID_W == 0 and dec_seq // GRID_W == 16

    cvec = jnp.concatenate([c_ctx[None, :], c, jnp.zeros((SUBLANES - 1 - dec_batch, D_MODEL), F32)], axis=0)
    mods = _adaln(cvec, ada_w, ada_b)

    prompt_row = lambda i: 0
    sample_row = lambda i: 1 + i
    xp = x_prompt.reshape(batch * seq, D_MODEL)
    xs = x_sample.reshape(dec_batch * dec_seq, D_MODEL)

    def sc_layer(xp, xs, layer, a):
        args = (norm_mix_g[layer][None, :], sc_w_in[a].astype(BF16), sc_conv_w[a], sc_conv_b[a][None, :],
                sc_w_out[a].astype(BF16))
        return (_sc_mixer(xp, mods, layer, prompt_row, *args, tile=ROWS_PROMPT_TILE, seq_len=seq),
                _sc_mixer(xs, mods, layer, sample_row, *args, tile=dec_seq, seq_len=dec_seq))

    def ffn_layer(xp, xs, layer, fg):
        args = (norm_ffn_g[layer][None, :], ffn_w_up[layer].astype(BF16), ffn_conv_w[layer],
                ffn_conv_b[layer][None, :], ffn_w_down[layer].astype(BF16))
        return (_ffn(xp, mods, layer, prompt_row, *args, fg, tile=ROWS_PROMPT_TILE, seq_len=seq),
                _ffn(xs, mods, layer, sample_row, *args, fg, tile=dec_seq, seq_len=dec_seq))

    depth = ada_w.shape[0]
    new_k, new_v = [], []
    for layer in range(depth):
        if layer % 2 == 0:
            xp, xs = sc_layer(xp, xs, layer, layer // 2)
        else:
            b = layer // 2
            g = norm_mix_g[layer][None, :]
            w_qkv = na_w_qkv[b].astype(BF16)
            w_o = na_w_o[b].astype(BF16)
            xp3, k_new, v_new = _ctx_mixer(xp.reshape(batch, seq, D_MODEL), mods, layer, g, w_qkv, w_o)
            new_k.append(k_new)
            new_v.append(v_new)
            xs3 = _na_mixer(xs.reshape(dec_batch, dec_seq, D_MODEL), mods, layer, g, w_qkv, w_o,
                            _pair_layout(cache_k_ctx[:, b]), _pair_layout(cache_v_ctx[:, b]),
                            _na_bias_table(na_rpb[b]))
            xp = xp3.reshape(batch * seq, D_MODEL)
            xs = xs3.reshape(dec_batch * dec_seq, D_MODEL)
        fg = final_g[None, :] if layer == depth - 1 else None
        xp, xs = ffn_layer(xp, xs, layer, fg)

    y_prompt = xp.reshape(batch, seq, D_MODEL)
    y_sample = xs.reshape(dec_batch, dec_seq, D_MODEL)
    return (y_prompt, y_sample, jnp.concatenate(new_k, axis=1), jnp.concatenate(new_v, axis=1))
```

```python
import functools

import jax
import jax.numpy as jnp
from jax import lax
from jax.experimental import pallas as pl
from jax.experimental.pallas import tpu as pltpu

F32 = jnp.float32
BF16 = jnp.bfloat16

D_MODEL = 1024
N_HEADS = 16
HEAD_DIM = D_MODEL // N_HEADS
GRID_W = 64
WIN_H = 8
WIN_W = 16
D_FF = 2816
EPS = 1e-6
NEG_INF = -1e30
N_MOD = 6
N_PAIRS = N_HEADS // 2
PAIR_W = 2 * HEAD_DIM
Q_SCALE = HEAD_DIM ** -0.5

VMEM_LIMIT_BYTES = 56 * 1024 * 1024
SUBLANES = 8

ADALN_TN = 1536
ROWS_PROMPT_TILE = 512
FFN_CHUNKS = ((0, 512), (512, 512), (1024, 512), (1536, 512), (2048, 512), (2560, 256))
SC_CHUNK = 512
NA_KEY_ROWS = WIN_H


def _dot(a, b):
    return jnp.dot(a, b, preferred_element_type=F32)


def _dot_nt(a, b):
    return lax.dot_general(a, b, (((1,), (1,)), ((), ())), preferred_element_type=F32)


def _rmsnorm(x, g):
    ms = jnp.mean(x * x, axis=-1, keepdims=True)
    return (x * lax.rsqrt(ms + EPS)) * g


def _norm_mod(x, g, shift, scale):
    return _rmsnorm(x, g) * (1.0 + scale) + shift


def _seq_edges(rows, seq_len):
    pos = lax.broadcasted_iota(jnp.int32, (rows, 1), 0) & (seq_len - 1)
    return pos == 0, pos == seq_len - 1


def _dwconv3(z, w, b, is_first, is_last):
    rows = z.shape[0]
    z_prev = jnp.where(is_first, 0.0, pltpu.roll(z, 1, 0))
    z_next = jnp.where(is_last, 0.0, pltpu.roll(z, rows - 1, 0))
    return z_prev * w[0:1, :] + z * w[1:2, :] + z_next * w[2:3, :] + b


def _const_spec(shape):
    nd = len(shape)
    return pl.BlockSpec(shape, lambda *_: (0,) * nd, pipeline_mode=pl.Buffered(1))


def _params(n_axes):
    return pltpu.CompilerParams(
        dimension_semantics=("arbitrary",) * n_axes,
        vmem_limit_bytes=VMEM_LIMIT_BYTES,
    )


def _adaln_kernel(c_ref, w_ref, b_ref, o_ref):
    s = jax.nn.silu(c_ref[...]).astype(BF16)
    o_ref[...] = _dot(s, w_ref[...].astype(BF16)) + b_ref[...]


def _adaln(cvec, ada_w, ada_b):
    depth = ada_w.shape[0]
    n_out = N_MOD * D_MODEL
    out = pl.pallas_call(
        _adaln_kernel,
        grid=(depth, n_out // ADALN_TN),
        in_specs=[
            pl.BlockSpec((SUBLANES, D_MODEL), lambda l, j: (0, 0)),
            pl.BlockSpec((None, D_MODEL, ADALN_TN), lambda l, j: (l, 0, j)),
            pl.BlockSpec((None, 1, ADALN_TN), lambda l, j: (l, 0, j)),
        ],
        out_specs=pl.BlockSpec((None, SUBLANES, ADALN_TN), lambda l, j: (l, 0, j)),
        out_shape=jax.ShapeDtypeStruct((depth, SUBLANES, n_out), F32),
        compiler_params=_params(2),
        name="adaln",
    )(cvec, ada_w, ada_b.reshape(depth, 1, n_out))
    return out.reshape(depth, SUBLANES, N_MOD, D_MODEL)


def _sc_mixer_kernel(x_ref, mod_ref, g_ref, w_in_ref, cw_ref, cb_ref, w_out_ref, o_ref, y_scr, *, seq_len):
    x = x_ref[...]
    rows = x.shape[0]
    h = _norm_mod(x, g_ref[...], mod_ref[0:1, :], mod_ref[1:2, :]).astype(BF16)
    is_first, is_last = _seq_edges(rows, seq_len)
    for off in range(0, D_MODEL, SC_CHUNK):
        sl = slice(off, off + SC_CHUNK)
        bg = _dot(h, w_in_ref[:, off:off + SC_CHUNK])
        cg = _dot(h, w_in_ref[:, D_MODEL + off:D_MODEL + off + SC_CHUNK])
        xv = _dot(h, w_in_ref[:, 2 * D_MODEL + off:2 * D_MODEL + off + SC_CHUNK])
        conv = _dwconv3(cg * xv, cw_ref[:, sl], cb_ref[:, sl], is_first, is_last)
        y_scr[:, sl] = (bg * conv).astype(BF16)
    y = _dot(y_scr[...], w_out_ref[...])
    o_ref[...] = x + mod_ref[2:3, :] * y


def _sc_mixer(x2d, mods, layer, mod_row, g, w_in, cw, cb, w_out, *, tile, seq_len):
    rows = x2d.shape[0]
    kern = functools.partial(_sc_mixer_kernel, seq_len=seq_len)
    return pl.pallas_call(
        kern,
        grid=(rows // tile,),
        in_specs=[
            pl.BlockSpec((tile, D_MODEL), lambda i: (i, 0)),
            pl.BlockSpec((None, None, N_MOD, D_MODEL), lambda i: (layer, mod_row(i), 0, 0)),
            _const_spec((1, D_MODEL)),
            _const_spec((D_MODEL, 3 * D_MODEL)),
            _const_spec((3, D_MODEL)),
            _const_spec((1, D_MODEL)),
            _const_spec((D_MODEL, D_MODEL)),
        ],
        out_specs=pl.BlockSpec((tile, D_MODEL), lambda i: (i, 0)),
        out_shape=jax.ShapeDtypeStruct((rows, D_MODEL), F32),
        scratch_shapes=[pltpu.VMEM((tile, D_MODEL), BF16)],
        compiler_params=_params(1),
        name="sc_mixer",
    )(x2d, mods, g, w_in, cw, cb, w_out)


def _ffn_kernel(x_ref, mod_ref, g_ref, w_up_ref, cw_ref, cb_ref, w_dn_ref, *rest, seq_len, final):
    if final:
        fg_ref, o_ref, acc_scr = rest
    else:
        o_ref, acc_scr = rest
    x = x_ref[...]
    rows = x.shape[0]
    h = _norm_mod(x, g_ref[...], mod_ref[3:4, :], mod_ref[4:5, :]).astype(BF16)
    is_first, is_last = _seq_edges(rows, seq_len)
    for idx, (off, width) in enumerate(FFN_CHUNKS):
        u = _dot(h, w_up_ref[:, off:off + width])
        gate = _dot(h, w_up_ref[:, D_FF + off:D_FF + off + width])
        u = _dwconv3(u, cw_ref[:, off:off + width], cb_ref[:, off:off + width], is_first, is_last)
        act = (jax.nn.gelu(u) * gate).astype(BF16)
        part = _dot(act, w_dn_ref[off:off + width, :])
        if idx == 0:
            acc_scr[...] = part
        else:
            acc_scr[...] += part
    xn = x + mod_ref[5:6, :] * acc_scr[...]
    o_ref[...] = _rmsnorm(xn, fg_ref[...]) if final else xn


def _ffn(x2d, mods, layer, mod_row, g, w_up, cw, cb, w_dn, final_g, *, tile, seq_len):
    rows = x2d.shape[0]
    final = final_g is not None
    kern = functools.partial(_ffn_kernel, seq_len=seq_len, final=final)
    in_specs = [
        pl.BlockSpec((tile, D_MODEL), lambda i: (i, 0)),
        pl.BlockSpec((None, None, N_MOD, D_MODEL), lambda i: (layer, mod_row(i), 0, 0)),
        _const_spec((1, D_MODEL)),
        _const_spec((D_MODEL, 2 * D_FF)),
        _const_spec((3, D_FF)),
        _const_spec((1, D_FF)),
        _const_spec((D_FF, D_MODEL)),
    ]
    args = [x2d, mods, g, w_up, cw, cb, w_dn]
    if final:
        in_specs.append(_const_spec((1, D_MODEL)))
        args.append(final_g)
    return pl.pallas_call(
        kern,
        grid=(rows // tile,),
        in_specs=in_specs,
        out_specs=pl.BlockSpec((tile, D_MODEL), lambda i: (i, 0)),
        out_shape=jax.ShapeDtypeStruct((rows, D_MODEL), F32),
        scratch_shapes=[pltpu.VMEM((tile, D_MODEL), F32)],
        compiler_params=_params(1),
        name="conv_ffn",
    )(*args)


def _lane_is_first_head():
    return lax.broadcasted_iota(jnp.int32, (1, PAIR_W), 1) < HEAD_DIM


def _stack_heads(q2, first):
    zero = jnp.zeros_like(q2)
    return jnp.concatenate([jnp.where(first, q2, zero), jnp.where(first, zero, q2)], axis=0)


def _unstack_heads(o, first):
    m = o.shape[0] // 2
    return jnp.where(first, o[:m], o[m:])


def _softmax_pv(scores, values):
    m = functools.reduce(jnp.maximum, [jnp.max(s, axis=-1, keepdims=True) for s in scores])
    es = [jnp.exp(s - m) for s in scores]
    denom = functools.reduce(jnp.add, [jnp.sum(e, axis=-1, keepdims=True) for e in es])
    out = functools.reduce(jnp.add, [_dot(e.astype(BF16), v) for e, v in zip(es, values)])
    return out * (1.0 / denom)


def _ctx_mixer_kernel(x_ref, mod_ref, g_ref, w_qkv_ref, w_o_ref, o_ref, k_ref, v_ref, o_scr):
    x = x_ref[...]
    h = _norm_mod(x, g_ref[...], mod_ref[0:1, :], mod_ref[1:2, :]).astype(BF16)
    q = _dot(h, w_qkv_ref[:, 0:D_MODEL]) * Q_SCALE
    k = _dot(h, w_qkv_ref[:, D_MODEL:2 * D_MODEL])
    v = _dot(h, w_qkv_ref[:, 2 * D_MODEL:3 * D_MODEL])
    for hd in range(N_HEADS):
        k_ref[hd] = k[:, hd * HEAD_DIM:(hd + 1) * HEAD_DIM]
        v_ref[hd] = v[:, hd * HEAD_DIM:(hd + 1) * HEAD_DIM]
    first = _lane_is_first_head()
    for p in range(N_PAIRS):
        sl = slice(p * PAIR_W, (p + 1) * PAIR_W)
        qs = _stack_heads(q[:, sl].astype(BF16), first)
        k2 = k[:, sl].astype(BF16)
        v2 = v[:, sl].astype(BF16)
        o = _softmax_pv([_dot_nt(qs, k2)], [v2])
        o_scr[:, sl] = _unstack_heads(o, first).astype(BF16)
    y = _dot(o_scr[...], w_o_ref[...])
    o_ref[...] = x + mod_ref[2:3, :] * y


def _ctx_mixer(x, mods, layer, g, w_qkv, w_o):
    batch, seq, _ = x.shape
    kv_shape = jax.ShapeDtypeStruct((batch, 1, N_HEADS, seq, HEAD_DIM), F32)
    kv_spec = pl.BlockSpec((None, None, N_HEADS, seq, HEAD_DIM), lambda b: (b, 0, 0, 0, 0))
    return pl.pallas_call(
        _ctx_mixer_kernel,
        grid=(batch,),
        in_specs=[
            pl.BlockSpec((None, seq, D_MODEL), lambda b: (b, 0, 0)),
            pl.BlockSpec((None, None, N_MOD, D_MODEL), lambda b: (layer, 0, 0, 0)),
            _const_spec((1, D_MODEL)),
            _const_spec((D_MODEL, 3 * D_MODEL)),
            _const_spec((D_MODEL, D_MODEL)),
        ],
        out_specs=[pl.BlockSpec((None, seq, D_MODEL), lambda b: (b, 0, 0)), kv_spec, kv_spec],
        out_shape=[jax.ShapeDtypeStruct(x.shape, F32), kv_shape, kv_shape],
        scratch_shapes=[pltpu.VMEM((seq, D_MODEL), BF16)],
        compiler_params=_params(1),
        name="ctx_mixer",
    )(x, mods, g, w_qkv, w_o)


def _na_bias_table(rpb):
    n_h, n_dr, n_dc = rpb.shape
    width = 2 * GRID_W
    pad_lo = GRID_W - WIN_W
    pad_hi = width - pad_lo - n_dc
    rpb = rpb.astype(F32)
    ext = jnp.concatenate([jnp.broadcast_to(rpb[..., :1], (n_h, n_dr, pad_lo)), rpb,
                           jnp.broadcast_to(rpb[..., -1:], (n_h, n_dr, pad_hi))], axis=-1)
    skew = jnp.broadcast_to(ext[:, :, None, :], (n_h, n_dr, GRID_W, width)).reshape(n_h, n_dr, GRID_W * width)
    skew = skew[..., :GRID_W * (width - 1)].reshape(n_h, n_dr, GRID_W, width - 1)
    toeplitz = skew[..., GRID_W - 1:2 * GRID_W - 1]
    qcol = jnp.arange(GRID_W)[:, None]
    kcol = jnp.arange(GRID_W)[None, :]
    col_start = jnp.clip(qcol - WIN_W // 2, 0, GRID_W - WIN_W)
    in_window = (kcol >= col_start) & (kcol < col_start + WIN_W)
    table = jnp.where(in_window[None, None], toeplitz, NEG_INF)
    table = jnp.concatenate([table[:, :-1], table[:, 1:]], axis=-1)
    table = table.reshape(N_PAIRS, 2, n_dr - 1, GRID_W, width).transpose(0, 2, 1, 3, 4)
    return table.reshape(N_PAIRS, n_dr - 1, 2 * GRID_W, width)


def _na_mixer_kernel(x_ref, mod_ref, g_ref, w_qkv_ref, w_o_ref, kc_ref, vc_ref, bias_ref, o_ref,
                     q_scr, k_scr, v_scr, o_scr, sctx_scr):
    p = pl.program_id(1)
    n_rows = x_ref.shape[0] // GRID_W

    @pl.when(p == 0)
    def _project():
        h = _norm_mod(x_ref[...], g_ref[...], mod_ref[0:1, :], mod_ref[1:2, :]).astype(BF16)
        for dst, base, scale in ((q_scr, 0, Q_SCALE), (k_scr, D_MODEL, None), (v_scr, 2 * D_MODEL, None)):
            t = _dot(h, w_qkv_ref[:, base:base + D_MODEL])
            if scale is not None:
                t = t * scale
            for j in range(N_PAIRS):
                dst[j] = t[:, j * PAIR_W:(j + 1) * PAIR_W].astype(BF16)

    first = _lane_is_first_head()
    kc2 = kc_ref[...]
    vc2 = vc_ref[...]
    seq = x_ref.shape[0]
    span = NA_KEY_ROWS * GRID_W
    sctx_scr[...] = _dot_nt(_stack_heads(q_scr[p], first), kc2)
    for r in range(n_rows):
        row0 = min(max(r - WIN_H // 2, 0), n_rows - WIN_H)
        d0 = row0 - r + (WIN_H - 1)
        q_rows = slice(r * GRID_W, (r + 1) * GRID_W)
        k_rows = slice(row0 * GRID_W, row0 * GRID_W + span)
        qs = _stack_heads(q_scr[p, q_rows, :], first)
        bias = jnp.concatenate([bias_ref[d0 + 2 * m] for m in range(NA_KEY_ROWS // 2)], axis=1)
        s_loc = _dot_nt(qs, k_scr[p, k_rows, :]) + bias
        s_ctx = jnp.concatenate([sctx_scr[q_rows, :], sctx_scr[seq + r * GRID_W:seq + (r + 1) * GRID_W, :]], axis=0)
        o = _softmax_pv([s_loc, s_ctx], [v_scr[p, k_rows, :], vc2])
        o_scr[p, q_rows, :] = _unstack_heads(o, first).astype(BF16)

    @pl.when(p == N_PAIRS - 1)
    def _output():
        o_all = jnp.concatenate([o_scr[j] for j in range(N_PAIRS)], axis=1)
        y = _dot(o_all, w_o_ref[...])
        o_ref[...] = x_ref[...] + mod_ref[2:3, :] * y


def _na_mixer(x, mods, layer, g, w_qkv, w_o, kc, vc, bias):
    batch, seq, _ = x.shape
    pair_scr = pltpu.VMEM((N_PAIRS, seq, PAIR_W), BF16)
    return pl.pallas_call(
        _na_mixer_kernel,
        grid=(batch, N_PAIRS),
        in_specs=[
            pl.BlockSpec((None, seq, D_MODEL), lambda b, p: (b, 0, 0)),
            pl.BlockSpec((None, None, N_MOD, D_MODEL), lambda b, p: (layer, 1 + b, 0, 0)),
            _const_spec((1, D_MODEL)),
            _const_spec((D_MODEL, 3 * D_MODEL)),
            _const_spec((D_MODEL, D_MODEL)),
            pl.BlockSpec((None, None) + kc.shape[2:], lambda b, p: (b, p, 0, 0)),
            pl.BlockSpec((None, None) + vc.shape[2:], lambda b, p: (b, p, 0, 0)),
            pl.BlockSpec((None,) + bias.shape[1:], lambda b, p: (p, 0, 0, 0)),
        ],
        out_specs=pl.BlockSpec((None, seq, D_MODEL), lambda b, p: (b, 0, 0)),
        out_shape=jax.ShapeDtypeStruct(x.shape, F32),
        scratch_shapes=[pair_scr, pair_scr, pair_scr, pair_scr, pltpu.VMEM((2 * seq, kc.shape[2]), F32)],
        compiler_params=_params(2),
        name="na_mixer",
    )(x, mods, g, w_qkv, w_o, kc, vc, bias)


def _pair_layout(cache):
    b, h, l, dh = cache.shape
    return cache.reshape(b, h // 2, 2, l, dh).transpose(0, 1, 3, 2, 4).reshape(b, h // 2, l, 2 * dh).astype(BF16)


def kernel(x_prompt, x_sample, cache_k_ctx, cache_v_ctx, c, c_ctx, ada_w, ada_b, norm_mix_g, norm_ffn_g,
           sc_w_in, sc_conv_w, sc_conv_b, sc_w_out, na_w_qkv, na_rpb, na_w_o, ffn_w_up, ffn_conv_w,
           ffn_conv_b, ffn_w_down, final_g):
    batch, seq, _ = x_prompt.shape
    dec_batch, dec_seq, _ = x_sample.shape
    assert ROWS_PROMPT_TILE % seq == 0 and dec_seq % GRID_W == 0 and dec_seq // GRID_W == 16

    cvec = jnp.concatenate([c_ctx[None, :], c, jnp.zeros((SUBLANES - 1 - dec_batch, D_MODEL), F32)], axis=0)
    mods = _adaln(cvec, ada_w, ada_b)

    prompt_row = lambda i: 0
    sample_row = lambda i: 1 + i
    xp = x_prompt.reshape(batch * seq, D_MODEL)
    xs = x_sample.reshape(dec_batch * dec_seq, D_MODEL)

    def sc_layer(xp, xs, layer, a):
        args = (norm_mix_g[layer][None, :], sc_w_in[a].astype(BF16), sc_conv_w[a], sc_conv_b[a][None, :],
                sc_w_out[a].astype(BF16))
        return (_sc_mixer(xp, mods, layer, prompt_row, *args, tile=ROWS_PROMPT_TILE, seq_len=seq),
                _sc_mixer(xs, mods, layer, sample_row, *args, tile=dec_seq, seq_len=dec_seq))

    def ffn_layer(xp, xs, layer, fg):
        args = (norm_ffn_g[layer][None, :], ffn_w_up[layer].astype(BF16), ffn_conv_w[layer],
                ffn_conv_b[layer][None, :], ffn_w_down[layer].astype(BF16))
        return (_ffn(xp, mods, layer, prompt_row, *args, fg, tile=ROWS_PROMPT_TILE, seq_len=seq),
                _ffn(xs, mods, layer, sample_row, *args, fg, tile=dec_seq, seq_len=dec_seq))

    depth = ada_w.shape[0]
    new_k, new_v = [], []
    for layer in range(depth):
        if layer % 2 == 0:
            xp, xs = sc_layer(xp, xs, layer, layer // 2)
        else:
            b = layer // 2
            g = norm_mix_g[layer][None, :]
            w_qkv = na_w_qkv[b].astype(BF16)
            w_o = na_w_o[b].astype(BF16)
            xp3, k_new, v_new = _ctx_mixer(xp.reshape(batch, seq, D_MODEL), mods, layer, g, w_qkv, w_o)
            new_k.append(k_new)
            new_v.append(v_new)
            xs3 = _na_mixer(xs.reshape(dec_batch, dec_seq, D_MODEL), mods, layer, g, w_qkv, w_o,
                            _pair_layout(cache_k_ctx[:, b]), _pair_layout(cache_v_ctx[:, b]),
                            _na_bias_table(na_rpb[b]))
            xp = xp3.reshape(batch * seq, D_MODEL)
            xs = xs3.reshape(dec_batch * dec_seq, D_MODEL)
        fg = final_g[None, :] if layer == depth - 1 else None
        xp, xs = ffn_layer(xp, xs, layer, fg)

    y_prompt = xp.reshape(batch, seq, D_MODEL)
    y_sample = xs.reshape(dec_batch, dec_seq, D_MODEL)
    return (y_prompt, y_sample, jnp.concatenate(new_k, axis=1), jnp.concatenate(new_v, axis=1))
```
